```python
import math
import jax
import jax.numpy as jnp
from jax import lax
import numpy as np

D_MODEL = 4096
BATCH = 1
SEQ = 8192
DEPTH = 2

RET_HEADS = 8
RET_DK = 128
RET_DV = 256
RET_CHUNK = 128
GDN_HEADS = 16
GDN_DK = 128
GDN_DV = 128
GDN_CONV = 4
GDN_CHUNK = 64
MLA_HEADS = 16
MLA_Q_LORA = 1024
MLA_KV_LORA = 512
MLA_NOPE = 128
MLA_ROPE = 64
MLA_DV = 128
MLA_QBLOCK = 128
ROPE_THETA = 10000.0
N_GROUPS = 4
EXPERTS_PER_GROUP = 8
N_EXPERTS = N_GROUPS * EXPERTS_PER_GROUP
TOP_K = 2
D_EXPERT = 512
MOE_BLOCK = 128
DEEPNORM_ALPHA = (2 * DEPTH) ** 0.25
DEEPNORM_BETA = (8 * DEPTH) ** -0.25
LN_EPS = 1e-5
RMS_EPS = 1e-6

RET_W = RET_HEADS * RET_DV
GDN_W = GDN_HEADS * GDN_DV
MLA_W = MLA_HEADS * MLA_DV
GDN_QKV = 2 * GDN_HEADS * GDN_DK + GDN_W
N_BRANCH = 3
IN_WIDTHS = (
    RET_HEADS * RET_DK,
    RET_HEADS * RET_DK,
    RET_W,
    RET_W,
    GDN_QKV,
    GDN_W,
    GDN_HEADS,
    GDN_HEADS,
    MLA_Q_LORA,
    MLA_KV_LORA,
    MLA_ROPE,
    N_BRANCH * D_MODEL,
)
D_IN = sum(IN_WIDTHS)

kernel_name = 'hybrid_ret_gdn_mla_hmoe_deepnorm'


def _layer_norm(x, g, b):
    xf = x.astype(jnp.float32)
    mu = jnp.mean(xf, -1, keepdims=True)
    var = jnp.mean(jnp.square(xf - mu), -1, keepdims=True)
    return ((xf - mu) * lax.rsqrt(var + LN_EPS) * g + b).astype(x.dtype)


def _rms_norm(x, g):
    xf = x.astype(jnp.float32)
    return (xf * lax.rsqrt(jnp.mean(xf * xf, -1, keepdims=True) + RMS_EPS) * g).astype(x.dtype)


def _l2norm(x):
    xf = x.astype(jnp.float32)
    return xf * lax.rsqrt(jnp.sum(xf * xf, -1, keepdims=True) + 1e-6)


def _rope_cos_sin(positions, dim):
    inv_freq = ROPE_THETA ** (-jnp.arange(0, dim, 2, dtype=jnp.float32) / dim)
    ang = positions.astype(jnp.float32)[..., None] * inv_freq
    return jnp.cos(ang)[:, :, None, :], jnp.sin(ang)[:, :, None, :]


def _apply_rope(x, cos, sin):
    xf = x.astype(jnp.float32)
    x1, x2 = jnp.split(xf, 2, axis=-1)
    return jnp.concatenate([x1 * cos - x2 * sin, x2 * cos + x1 * sin], -1).astype(x.dtype)


def _causal_dwconv(x, w):
    k, c = w.shape
    return lax.conv_general_dilated(x, w[:, None, :], window_strides=(1,), padding=[(k - 1, 0)],
                                    dimension_numbers=('NWC', 'WIO', 'NWC'), feature_group_count=c)


def _retention(q, k, v):
    b, s, h, dk = q.shape
    c = RET_CHUNK
    n = s // c
    log_gamma = jnp.log(1.0 - 2.0 ** (-5.0 - jnp.arange(h, dtype=jnp.float32)))

    def chunk(t):
        return t.astype(jnp.float32).reshape(b, n, c, h, -1).transpose(0, 3, 1, 2, 4)

    q, k, v = chunk(q), chunk(k) * dk ** -0.5, chunk(v)
    pos = jnp.arange(c, dtype=jnp.float32)
    rel = pos[:, None] - pos[None, :]
    decay = jnp.where(rel >= 0, jnp.exp(jnp.maximum(rel, 0.0)[None] * log_gamma[:, None, None]), 0.0)
    scores = jnp.einsum('bhncd,bhnsd->bhncs', q, k) * decay[None, :, None]
    inner = jnp.einsum('bhncs,bhnsv->bhncv', scores, v)
    k_to_end = jnp.exp((c - 1 - pos)[None, :] * log_gamma[:, None])
    chunk_kv = jnp.einsum('bhncd,bhncv->nbhdv', k * k_to_end[None, :, None, :, None], v)
    chunk_decay = jnp.exp(c * log_gamma)[None, :, None, None]

    def step(state, kv):
        return state * chunk_decay + kv, state

    _, prev = lax.scan(step, jnp.zeros(chunk_kv.shape[1:], jnp.float32), chunk_kv)
    q_from_start = jnp.exp((pos + 1.0)[None, :] * log_gamma[:, None])
    cross = jnp.einsum('bhncd,nbhdv->bhncv', q * q_from_start[None, :, None, :, None], prev)
    out = inner + cross
    return out.transpose(0, 2, 3, 1, 4).reshape(b, s, h, -1)


def _gated_delta_rule(q, k, v, g, beta):
    b, s, h, dk = q.shape
    dv = v.shape[-1]
    c = GDN_CHUNK
    n = s // c
    f32 = jnp.float32

    def chunk(t):
        return t.astype(f32).reshape(b, n, c, h, -1).transpose(0, 3, 1, 2, 4)

    q = chunk(_l2norm(q) * dk ** -0.5)
    k = chunk(_l2norm(k))
    v = chunk(v)
    g = g.astype(f32).reshape(b, n, c, h).transpose(0, 3, 1, 2)
    beta = beta.astype(f32).reshape(b, n, c, h).transpose(0, 3, 1, 2)
    gc = jnp.cumsum(g, axis=-1)
    causal = jnp.tril(jnp.ones((c, c), bool))
    strict = jnp.tril(jnp.ones((c, c), bool), -1)
    decay = jnp.exp(jnp.where(causal, gc[..., :, None] - gc[..., None, :], -jnp.inf))
    kb = k * beta[..., None]
    a = jnp.where(strict, jnp.einsum('bhncd,bhnsd->bhncs', kb, k) * decay, 0.0)
    eye = jnp.broadcast_to(jnp.eye(c, dtype=f32), a.shape)
    t = lax.linalg.triangular_solve(eye + a, eye, left_side=True, lower=True, unit_diagonal=True)
    u = t @ (v * beta[..., None])
    w = t @ (kb * jnp.exp(gc)[..., None])
    qk = jnp.where(causal, jnp.einsum('bhncd,bhnsd->bhncs', q, k) * decay, 0.0)
    qg = q * jnp.exp(gc)[..., None]
    kg = k * jnp.exp(gc[..., -1:] - gc)[..., None]
    gl = jnp.exp(gc[..., -1])[..., None, None]

    def step(state, xs):
        u_i, w_i, qk_i, qg_i, kg_i, gl_i = xs
        v_new = u_i - w_i @ state
        o = qg_i @ state + qk_i @ v_new
        state = state * gl_i + jnp.swapaxes(kg_i, -1, -2) @ v_new
        return state, o

    xs = (jnp.moveaxis(u, 2, 0), jnp.moveaxis(w, 2, 0), jnp.moveaxis(qk, 2, 0),
          jnp.moveaxis(qg, 2, 0), jnp.moveaxis(kg, 2, 0), jnp.moveaxis(gl, 2, 0))
    _, o = lax.scan(step, jnp.zeros((b, h, dk, dv), f32), xs)
    return o.transpose(1, 0, 3, 2, 4).reshape(b, s, h, dv)


def _mla(q_a, kv_a, k_pe_raw, cos, sin, q_norm_g, w_qb, kv_norm_g, w_kvb):
    b, s, _ = q_a.shape
    h = MLA_HEADS
    q = (_rms_norm(q_a, q_norm_g) @ w_qb).reshape(b, s, h, MLA_NOPE + MLA_ROPE)
    q_nope, q_pe = q[..., :MLA_NOPE], _apply_rope(q[..., MLA_NOPE:], cos, sin)
    kv = (_rms_norm(kv_a, kv_norm_g) @ w_kvb).reshape(b, s, h, MLA_NOPE + MLA_DV)
    k_nope, v = kv[..., :MLA_NOPE], kv[..., MLA_NOPE:]
    k_pe = _apply_rope(k_pe_raw[:, :, None, :], cos, sin)
    qf = jnp.concatenate([q_nope, q_pe], -1)
    kf = jnp.concatenate([k_nope, jnp.broadcast_to(k_pe, (b, s, h, MLA_ROPE))], -1)
    scale = (MLA_NOPE + MLA_ROPE) ** -0.5
    nb = s // MLA_QBLOCK
    q_blocks = qf.reshape(b, nb, MLA_QBLOCK, h, -1).transpose(1, 0, 2, 3, 4)
    key_pos = jnp.arange(s)

    def block(args):
        qb, bi = args
        sc = jnp.einsum('bqhd,bkhd->bhqk', qb, kf).astype(jnp.float32) * scale
        q_pos = bi * MLA_QBLOCK + jnp.arange(MLA_QBLOCK)
        sc = jnp.where(key_pos[None, :] <= q_pos[:, None], sc, -jnp.inf)
        p = jax.nn.softmax(sc, axis=-1).astype(v.dtype)
        return jnp.einsum('bhqk,bkhv->bqhv', p, v)

    o = lax.map(block, (q_blocks, jnp.arange(nb)))
    return o.transpose(1, 0, 2, 3, 4).reshape(b, s, h * MLA_DV)


def _token_mixer(x, cos_ret, sin_ret, cos_mla, sin_mla, w_in, gdn_conv_w, gdn_a_log, gdn_dt_bias,
                 gdn_norm_g, mla_q_norm_g, mla_w_qb, mla_kv_norm_g, mla_w_kvb,
                 w_br_ret, w_br_gdn, w_br_mla, w_out):
    b, s, _ = x.shape
    proj = x @ w_in
    splits = np.cumsum(IN_WIDTHS)[:-1].tolist()
    (r_q, r_k, r_v, r_g, g_qkv, g_z, g_a, g_b, m_qa, m_kva, m_kpe, gates) = jnp.split(proj, splits, axis=-1)

    rq = _apply_rope(r_q.reshape(b, s, RET_HEADS, RET_DK), cos_ret, sin_ret)
    rk = _apply_rope(r_k.reshape(b, s, RET_HEADS, RET_DK), cos_ret, sin_ret)
    ro = _retention(rq, rk, r_v.reshape(b, s, RET_HEADS, RET_DV))
    mu = jnp.mean(ro, -1, keepdims=True)
    var = jnp.mean(jnp.square(ro - mu), -1, keepdims=True)
    ro = ((ro - mu) * lax.rsqrt(var + LN_EPS)).reshape(b, s, RET_W)
    ro = (ro * jax.nn.silu(r_g.astype(jnp.float32))).astype(x.dtype)

    qkv = jax.nn.silu(_causal_dwconv(g_qkv, gdn_conv_w))
    gq, gk, gv = jnp.split(qkv, [GDN_HEADS * GDN_DK, 2 * GDN_HEADS * GDN_DK], axis=-1)
    log_decay = -jnp.exp(gdn_a_log.astype(jnp.float32)) * jax.nn.softplus(g_a.astype(jnp.float32) + gdn_dt_bias)
    beta = jax.nn.sigmoid(g_b.astype(jnp.float32))
    go = _gated_delta_rule(gq.reshape(b, s, GDN_HEADS, GDN_DK), gk.reshape(b, s, GDN_HEADS, GDN_DK),
                           gv.reshape(b, s, GDN_HEADS, GDN_DV), log_decay, beta)
    go = _rms_norm(go, gdn_norm_g) * jax.nn.silu(g_z.reshape(b, s, GDN_HEADS, GDN_DV).astype(jnp.float32))
    go = go.reshape(b, s, GDN_W).astype(x.dtype)

    mo = _mla(m_qa, m_kva, m_kpe, cos_mla, sin_mla, mla_q_norm_g, mla_w_qb, mla_kv_norm_g, mla_w_kvb).astype(x.dtype)

    gate_r, gate_g, gate_m = jnp.split(jax.nn.sigmoid(gates), N_BRANCH, axis=-1)
    merged = gate_r * (ro @ w_br_ret) + gate_g * (go @ w_br_gdn) + gate_m * (mo @ w_br_mla)
    return merged @ w_out


def _hier_moe(x, w_group, b_group, w_expert, b_expert, w_gate, w_up, w_down):
    b, s, d = x.shape
    n = b * s
    xt = x.reshape(n, d)
    xf = xt.astype(jnp.float32)
    group_prob = jax.nn.softmax(xf @ w_group.astype(jnp.float32) + b_group.astype(jnp.float32), axis=-1)
    p_group, g_sel = lax.top_k(group_prob, 1)
    expert_logits = (xf @ w_expert.astype(jnp.float32) + b_expert.astype(jnp.float32)).reshape(n, N_GROUPS, EXPERTS_PER_GROUP)
    sel_idx = jnp.broadcast_to(g_sel[:, :, None], (n, 1, EXPERTS_PER_GROUP))
    in_group = jnp.take_along_axis(expert_logits, sel_idx, axis=1)[:, 0]
    top_logit, top_local = lax.top_k(in_group, TOP_K)
    weights = p_group * jax.nn.softmax(top_logit, axis=-1)
    expert_id = g_sel * EXPERTS_PER_GROUP + top_local

    a = n * TOP_K
    flat_e = expert_id.reshape(a)
    flat_tok = jnp.repeat(jnp.arange(n, dtype=jnp.int32), TOP_K)
    flat_w = weights.reshape(a)
    order = jnp.argsort(flat_e)
    se, st, sw = flat_e[order], flat_tok[order], flat_w[order]
    counts = jnp.bincount(flat_e, length=N_EXPERTS)
    padded = (counts + MOE_BLOCK - 1) // MOE_BLOCK * MOE_BLOCK
    pad_end = jnp.cumsum(padded)
    pad_start = pad_end - padded
    start = jnp.cumsum(counts) - counts
    dest = pad_start[se] + jnp.arange(a) - start[se]
    n_blocks = -(-a // MOE_BLOCK) + N_EXPERTS
    p_rows = n_blocks * MOE_BLOCK
    buf_tok = jnp.zeros((p_rows,), jnp.int32).at[dest].set(st)
    buf_w = jnp.zeros((p_rows,), jnp.float32).at[dest].set(sw)
    block_expert = jnp.minimum(jnp.searchsorted(pad_end, jnp.arange(n_blocks) * MOE_BLOCK, side='right'), N_EXPERTS - 1)

    def run_block(args):
        tok, wt, e = args
        xb = xt[tok]
        hdn = jax.nn.silu(xb @ w_gate[e]) * (xb @ w_up[e])
        return (hdn @ w_down[e]) * wt[:, None].astype(x.dtype)

    yb = lax.map(run_block, (buf_tok.reshape(n_blocks, MOE_BLOCK), buf_w.reshape(n_blocks, MOE_BLOCK), block_expert))
    y = jax.ops.segment_sum(yb.reshape(p_rows, d), buf_tok, num_segments=n)
    return y.reshape(b, s, d)


def setup_inputs(seed: int = 0) -> dict:
    key = jax.random.key(seed)
    ks = jax.random.split(key, 26)
    f32 = jnp.float32
    L = DEPTH

    def nrm(k, shape, scale):
        return jax.random.normal(k, shape, f32) * scale

    x = nrm(ks[0], (BATCH, SEQ, D_MODEL), 1.0)
    positions = jnp.broadcast_to(jnp.arange(SEQ, dtype=jnp.int32)[None, :], (BATCH, SEQ))
    w_in = nrm(ks[1], (L, D_MODEL, D_IN), D_MODEL ** -0.5)
    gdn_conv_w = nrm(ks[2], (L, GDN_CONV, GDN_QKV), GDN_CONV ** -0.5)
    gdn_a_log = jnp.log(jax.random.uniform(ks[3], (L, GDN_HEADS), f32, minval=1.0, maxval=16.0))
    dt = jnp.exp(jax.random.uniform(ks[4], (L, GDN_HEADS), f32, minval=math.log(1e-3), maxval=math.log(1e-1)))
    gdn_dt_bias = dt + jnp.log(-jnp.expm1(-dt))
    gdn_norm_g = 1.0 + nrm(ks[5], (L, GDN_DV), 0.02)
    mla_q_norm_g = 1.0 + nrm(ks[6], (L, MLA_Q_LORA), 0.02)
    mla_w_qb = nrm(ks[7], (L, MLA_Q_LORA, MLA_HEADS * (MLA_NOPE + MLA_ROPE)), MLA_Q_LORA ** -0.5)
    mla_kv_norm_g = 1.0 + nrm(ks[8], (L, MLA_KV_LORA), 0.02)
    mla_w_kvb = nrm(ks[9], (L, MLA_KV_LORA, MLA_HEADS * (MLA_NOPE + MLA_DV)), MLA_KV_LORA ** -0.5)
    w_br_ret = nrm(ks[10], (L, RET_W, D_MODEL), RET_W ** -0.5 * DEEPNORM_BETA)
    w_br_gdn = nrm(ks[11], (L, GDN_W, D_MODEL), GDN_W ** -0.5 * DEEPNORM_BETA)
    w_br_mla = nrm(ks[12], (L, MLA_W, D_MODEL), MLA_W ** -0.5 * DEEPNORM_BETA)
    w_out = nrm(ks[13], (L, D_MODEL, D_MODEL), D_MODEL ** -0.5 * DEEPNORM_BETA)
    ln1_g = 1.0 + nrm(ks[14], (L, D_MODEL), 0.02)
    ln1_b = nrm(ks[15], (L, D_MODEL), 0.02)
    w_group = nrm(ks[16], (L, D_MODEL, N_GROUPS), D_MODEL ** -0.5)
    b_group = nrm(ks[17], (L, N_GROUPS), 0.01)
    w_expert = nrm(ks[18], (L, D_MODEL, N_EXPERTS), D_MODEL ** -0.5)
    b_expert = nrm(ks[19], (L, N_EXPERTS), 0.01)
    w_gate = nrm(ks[20], (L, N_EXPERTS, D_MODEL, D_EXPERT), D_MODEL ** -0.5)
    w_up = nrm(ks[21], (L, N_EXPERTS, D_MODEL, D_EXPERT), D_MODEL ** -0.5)
    w_down = nrm(ks[22], (L, N_EXPERTS, D_EXPERT, D_MODEL), D_EXPERT ** -0.5 * DEEPNORM_BETA)
    ln2_g = 1.0 + nrm(ks[23], (L, D_MODEL), 0.02)
    ln2_b = nrm(ks[24], (L, D_MODEL), 0.02)
    return {'x': x, 'positions': positions, 'w_in': w_in, 'gdn_conv_w': gdn_conv_w,
            'gdn_a_log': gdn_a_log, 'gdn_dt_bias': gdn_dt_bias, 'gdn_norm_g': gdn_norm_g,
            'mla_q_norm_g': mla_q_norm_g, 'mla_w_qb': mla_w_qb, 'mla_kv_norm_g': mla_kv_norm_g,
            'mla_w_kvb': mla_w_kvb, 'w_br_ret': w_br_ret, 'w_br_gdn': w_br_gdn, 'w_br_mla': w_br_mla,
            'w_out': w_out, 'ln1_g': ln1_g, 'ln1_b': ln1_b, 'w_group': w_group, 'b_group': b_group,
            'w_expert': w_expert, 'b_expert': b_expert, 'w_gate': w_gate, 'w_up': w_up,
            'w_down': w_down, 'ln2_g': ln2_g, 'ln2_b': ln2_b}


def reference(x, positions, w_in, gdn_conv_w, gdn_a_log, gdn_dt_bias, gdn_norm_g, mla_q_norm_g,
              mla_w_qb, mla_kv_norm_g, mla_w_kvb, w_br_ret, w_br_gdn, w_br_mla, w_out, ln1_g, ln1_b,
              w_group, b_group, w_expert, b_expert, w_gate, w_up, w_down, ln2_g, ln2_b):
    cos_ret, sin_ret = _rope_cos_sin(positions, RET_DK)
    cos_mla, sin_mla = _rope_cos_sin(positions, MLA_ROPE)
    h = x
    for l in range(DEPTH):
        mix = _token_mixer(h, cos_ret, sin_ret, cos_mla, sin_mla, w_in[l], gdn_conv_w[l], gdn_a_log[l],
                           gdn_dt_bias[l], gdn_norm_g[l], mla_q_norm_g[l], mla_w_qb[l], mla_kv_norm_g[l],
                           mla_w_kvb[l], w_br_ret[l], w_br_gdn[l], w_br_mla[l], w_out[l])
        h = _layer_norm(DEEPNORM_ALPHA * h + mix, ln1_g[l], ln1_b[l])
        ffn = _hier_moe(h, w_group[l], b_group[l], w_expert[l], b_expert[l], w_gate[l], w_up[l], w_down[l])
        h = _layer_norm(DEEPNORM_ALPHA * h + ffn, ln2_g[l], ln2_b[l])
    return h
```

```python
import functools
import math

import jax
import jax.numpy as jnp
from jax import lax
from jax.experimental import pallas as pl
from jax.experimental.pallas import tpu as pltpu

F32 = jnp.float32
BF16 = jnp.bfloat16
HIGHEST = lax.Precision.HIGHEST

DEPTH = 2
RET_HEADS, RET_DK, RET_DV, RET_CHUNK = 8, 128, 256, 128
GDN_HEADS, GDN_DK, GDN_DV, GDN_CONV, GDN_CHUNK = 16, 128, 128, 4, 64
MLA_HEADS, MLA_Q_LORA, MLA_KV_LORA, MLA_NOPE, MLA_ROPE, MLA_DV = 16, 1024, 512, 128, 64, 128
ROPE_THETA = 10000.0
N_GROUPS, EXPERTS_PER_GROUP, TOP_K, D_EXPERT = 4, 8, 2, 512
N_EXPERTS = N_GROUPS * EXPERTS_PER_GROUP
DEEPNORM_ALPHA = (2 * DEPTH) ** 0.25
LN_EPS = 1e-5
RMS_EPS = 1e-6
RET_W = RET_HEADS * RET_DV
GDN_W = GDN_HEADS * GDN_DV
MLA_W = MLA_HEADS * MLA_DV
GDN_QKV = 2 * GDN_HEADS * GDN_DK + GDN_W

LANES = 128
SUBLANES = 8
VMEM_LIMIT = 56 * 1024 * 1024

MLA_HEAD_W = 2 * LANES
MOE_ROWS = 256


def _cparams(sem):
    return pltpu.CompilerParams(dimension_semantics=sem, vmem_limit_bytes=VMEM_LIMIT)


def _dot(a, b, precision=None):
    return jnp.dot(a, b, preferred_element_type=F32, precision=precision)


def _dot_nt(a, b, precision=None):
    return lax.dot_general(a, b, (((1,), (1,)), ((), ())), preferred_element_type=F32, precision=precision)


def _dot_tn(a, b, precision=None):
    return lax.dot_general(a, b, (((0,), (0,)), ((), ())), preferred_element_type=F32, precision=precision)


def _sigmoid(x):
    return 1.0 / (1.0 + jnp.exp(-x))


def _silu(x):
    return x * _sigmoid(x)


def _mm_body(x_ref, w_ref, o_ref):
    o_ref[...] = _dot(x_ref[...], w_ref[...]).astype(o_ref.dtype)


def _matmul(x, w, out_dtype, tm=1024, tn=512):
    m, k = x.shape
    n = w.shape[1]
    tm, tn = min(tm, m), min(tn, n)
    assert m % tm == 0 and n % tn == 0, (m, n, tm, tn)
    return pl.pallas_call(
        _mm_body,
        grid=(m // tm, n // tn),
        in_specs=[pl.BlockSpec((tm, k), lambda i, j: (i, 0)),
                  pl.BlockSpec((k, tn), lambda i, j: (0, j))],
        out_specs=pl.BlockSpec((tm, tn), lambda i, j: (i, j)),
        out_shape=jax.ShapeDtypeStruct((m, n), out_dtype),
        compiler_params=_cparams(("parallel", "parallel")),
        name="matmul",
    )(x, w)


def _rope128(x, cosv, sinv):
    return x * cosv + pltpu.roll(x, 64, axis=1) * sinv


def _ret_body(lg_ref, q_ref, k_ref, v_ref, g_ref, cos_ref, sin_ref, o_ref, state_ref, *, n_sub):
    h = pl.program_id(0)

    @pl.when(pl.program_id(1) == 0)
    def _():
        state_ref[...] = jnp.zeros_like(state_ref)

    c = RET_CHUNK
    lg = lg_ref[h]
    ii = lax.broadcasted_iota(jnp.int32, (c, c), 0)
    jj = lax.broadcasted_iota(jnp.int32, (c, c), 1)
    rel = (ii - jj).astype(F32)
    decay = jnp.where(rel >= 0, jnp.exp(jnp.maximum(rel, 0.0) * lg), 0.0)
    pos = lax.broadcasted_iota(jnp.int32, (c, 1), 0).astype(F32)
    q_from_start = jnp.exp((pos + 1.0) * lg)
    k_to_end = jnp.exp((c - 1.0 - pos) * lg)
    chunk_decay = jnp.exp(jnp.full((1, 1), float(c), F32) * lg)

    for s in range(n_sub):
        rows = slice(s * c, (s + 1) * c)
        cosv, sinv = cos_ref[rows, :], sin_ref[rows, :]
        q = _rope128(q_ref[rows, :], cosv, sinv)
        k = _rope128(k_ref[rows, :], cosv, sinv) * (RET_DK ** -0.5)
        v = v_ref[rows, :].astype(BF16)
        state = state_ref[...]
        scores = _dot_nt(q.astype(BF16), k.astype(BF16)) * decay
        out = _dot(scores.astype(BF16), v)
        out = out + _dot((q * q_from_start).astype(BF16), state.astype(BF16))
        state_ref[...] = state * chunk_decay + _dot_tn((k * k_to_end).astype(BF16), v)
        mu = jnp.mean(out, axis=-1, keepdims=True)
        d = out - mu
        var = jnp.mean(d * d, axis=-1, keepdims=True)
        o_ref[rows, :] = (d * lax.rsqrt(var + LN_EPS) * _silu(g_ref[rows, :])).astype(o_ref.dtype)


def _retention(p_ret, cosv, sinv, log_gamma, rows_per_step=512):
    s = p_ret.shape[0]
    tr = min(rows_per_step, s)
    assert s % tr == 0 and tr % RET_CHUNK == 0
    kq, kv = RET_HEADS, (2 * RET_HEADS * RET_DK) // RET_DV
    return pl.pallas_call(
        functools.partial(_ret_body, n_sub=tr // RET_CHUNK),
        grid_spec=pltpu.PrefetchScalarGridSpec(
            num_scalar_prefetch=1,
            grid=(RET_HEADS, s // tr),
            in_specs=[pl.BlockSpec((tr, RET_DK), lambda h, r, lg: (r, h)),
                      pl.BlockSpec((tr, RET_DK), lambda h, r, lg: (r, kq + h)),
                      pl.BlockSpec((tr, RET_DV), lambda h, r, lg: (r, kv + h)),
                      pl.BlockSpec((tr, RET_DV), lambda h, r, lg: (r, kv + RET_HEADS + h)),
                      pl.BlockSpec((tr, RET_DK), lambda h, r, lg: (r, 0)),
                      pl.BlockSpec((tr, RET_DK), lambda h, r, lg: (r, 0))],
            out_specs=pl.BlockSpec((tr, RET_DV), lambda h, r, lg: (r, h)),
            scratch_shapes=[pltpu.VMEM((RET_DK, RET_DV), F32)]),
        out_shape=jax.ShapeDtypeStruct((s, RET_W), BF16),
        compiler_params=_cparams(("parallel", "arbitrary")),
        name="retention",
    )(log_gamma, p_ret, p_ret, p_ret, p_ret, cosv, sinv)


def _conv_silu(x, cw, tail_ref):
    tr = x.shape[0]
    tail = tail_ref[...]
    row8 = lax.broadcasted_iota(jnp.int32, (SUBLANES, x.shape[1]), 0)
    acc = x * cw[GDN_CONV - 1:GDN_CONV, :]
    for s in range(1, GDN_CONV):
        xr = pltpu.roll(x, s, axis=0)
        head = jnp.where(row8 < s, pltpu.roll(tail, s, axis=0), xr[:SUBLANES, :])
        xs = jnp.concatenate([head, xr[SUBLANES:, :]], axis=0)
        acc = acc + xs * cw[GDN_CONV - 1 - s:GDN_CONV - s, :]
    tail_ref[...] = x[tr - SUBLANES:, :]
    return _silu(acc)


def _softplus(x):
    return jnp.maximum(x, 0.0) + jnp.log1p(jnp.exp(-jnp.abs(x)))


def _gdn_body(alog_ref, dtb_ref, q_ref, k_ref, v_ref, z_ref, ab_ref, cwq_ref, cwk_ref, cwv_ref, ng_ref,
              o_ref, state_ref, tq_ref, tk_ref, tv_ref, *, n_sub, heads_per_step):
    hg = pl.program_id(0)

    @pl.when(pl.program_id(1) == 0)
    def _():
        state_ref[...] = jnp.zeros_like(state_ref)
        tq_ref[...] = jnp.zeros_like(tq_ref)
        tk_ref[...] = jnp.zeros_like(tk_ref)
        tv_ref[...] = jnp.zeros_like(tv_ref)

    c = GDN_CHUNK
    qa = _conv_silu(q_ref[...], cwq_ref[...], tq_ref)
    ka = _conv_silu(k_ref[...], cwk_ref[...], tk_ref)
    va = _conv_silu(v_ref[...], cwv_ref[...], tv_ref)
    ab = ab_ref[...]
    lane = lax.broadcasted_iota(jnp.int32, ab.shape, 1)

    ii = lax.broadcasted_iota(jnp.int32, (c, c), 0)
    jj = lax.broadcasted_iota(jnp.int32, (c, c), 1)
    causal = ii >= jj
    strict = ii > jj
    tril = causal.astype(F32)
    triu = (ii <= jj).astype(F32)
    ones = jnp.ones((c, c), F32)
    eye = (ii == jj).astype(F32)
    ng = ng_ref[...]

    for hh in range(heads_per_step):
        h = hg * heads_per_step + hh
        cols = slice(hh * GDN_DK, (hh + 1) * GDN_DK)
        neg_a = -jnp.exp(jnp.full((1, 1), alog_ref[h], F32))
        dtb = dtb_ref[h]
        a_all = jnp.sum(jnp.where(lane == h, ab, 0.0), axis=1, keepdims=True)
        b_all = jnp.sum(jnp.where(lane == GDN_HEADS + h, ab, 0.0), axis=1, keepdims=True)
        g_all = neg_a * _softplus(a_all + dtb)
        beta_all = _sigmoid(b_all)

        pre = []
        for s in range(n_sub):
            rows = slice(s * c, (s + 1) * c)
            q, k, v = qa[rows, cols], ka[rows, cols], va[rows, cols]
            q = q * lax.rsqrt(jnp.sum(q * q, axis=-1, keepdims=True) + 1e-6) * (GDN_DK ** -0.5)
            k = k * lax.rsqrt(jnp.sum(k * k, axis=-1, keepdims=True) + 1e-6)
            beta = beta_all[rows, :]
            gb = jnp.broadcast_to(g_all[rows, :], (c, c))
            gc_i = _dot(tril, gb, HIGHEST)
            gc_j = _dot(ones, gb * triu, HIGHEST)
            decay = jnp.where(causal, jnp.exp(jnp.minimum(gc_i - gc_j, 0.0)), 0.0)
            gc = gc_i[:, :1]
            gc_last = gc_i[c - 1:c, :1]
            kb = k * beta
            a = jnp.where(strict, _dot_nt(kb, k, HIGHEST) * decay, 0.0)
            t = eye - a
            p = a
            for _ in range(5):
                p = _dot(p, p, HIGHEST)
                t = t + _dot(t, p, HIGHEST)
            egc = jnp.exp(gc)
            u = _dot(t, v * beta, HIGHEST)
            w = _dot(t, kb * egc, HIGHEST)
            qk = jnp.where(causal, _dot_nt(q.astype(BF16), k.astype(BF16)) * decay, 0.0)
            qg = q * egc
            kg = k * jnp.exp(gc_last - gc)
            gl = jnp.exp(gc_last)
            pre.append((u, w.astype(BF16), qk.astype(BF16), qg.astype(BF16), kg.astype(BF16), gl))

        state = state_ref[hh]
        for s in range(n_sub):
            rows = slice(s * c, (s + 1) * c)
            u, w, qk, qg, kg, gl = pre[s]
            sb = state.astype(BF16)
            v_new = u - _dot(w, sb)
            vb = v_new.astype(BF16)
            o = _dot(qg, sb) + _dot(qk, vb)
            state = state * gl + _dot_tn(kg, vb)
            on = o * lax.rsqrt(jnp.mean(o * o, axis=-1, keepdims=True) + RMS_EPS) * ng
            o_ref[rows, cols] = (on * _silu(z_ref[rows, cols])).astype(o_ref.dtype)
        state_ref[hh] = state


def _gdn(p_gdn, p_z, p_small, ab_block, conv_w, a_log, dt_bias, norm_g, rows_per_step=256, heads_per_step=2):
    s = p_gdn.shape[0]
    tr = min(rows_per_step, s)
    hb = heads_per_step
    assert s % tr == 0 and tr % GDN_CHUNK == 0 and GDN_HEADS % hb == 0
    w = hb * GDN_DK
    nq = (GDN_HEADS * GDN_DK) // w
    spec = lambda off: pl.BlockSpec((tr, w), lambda h, r, *_: (r, off + h))
    cspec = lambda off: pl.BlockSpec((GDN_CONV, w), lambda h, r, *_: (0, off + h))
    return pl.pallas_call(
        functools.partial(_gdn_body, n_sub=tr // GDN_CHUNK, heads_per_step=hb),
        grid_spec=pltpu.PrefetchScalarGridSpec(
            num_scalar_prefetch=2,
            grid=(GDN_HEADS // hb, s // tr),
            in_specs=[spec(0), spec(nq), spec(2 * nq), spec(0),
                      pl.BlockSpec((tr, LANES), lambda h, r, *_: (r, ab_block)),
                      cspec(0), cspec(nq), cspec(2 * nq),
                      pl.BlockSpec((1, GDN_DV), lambda h, r, *_: (0, 0))],
            out_specs=spec(0),
            scratch_shapes=[pltpu.VMEM((hb, GDN_DK, GDN_DV), F32),
                            pltpu.VMEM((SUBLANES, w), F32),
                            pltpu.VMEM((SUBLANES, w), F32),
                            pltpu.VMEM((SUBLANES, w), F32)]),
        out_shape=jax.ShapeDtypeStruct((s, GDN_W), BF16),
        compiler_params=_cparams(("parallel", "arbitrary")),
        name="gated_deltanet",
    )(a_log, dt_bias, p_gdn, p_gdn, p_gdn, p_z, p_small, conv_w, conv_w, conv_w, norm_g.reshape(1, GDN_DV))


def _rms(x, g):
    return x * lax.rsqrt(jnp.mean(x * x, axis=-1, keepdims=True) + RMS_EPS) * g


def _mla_prep_body(qa_ref, kva_ref, kpe_ref, cos_ref, sin_ref, qg_ref, kvg_ref, wq_ref, wk_ref, wv_ref,
                   q_ref, k_ref, v_ref):
    cosv, sinv = cos_ref[...], sin_ref[...]
    scale = (MLA_NOPE + MLA_ROPE) ** -0.5
    qn = _rms(qa_ref[...], qg_ref[...]).astype(BF16)
    q = _dot(qn, wq_ref[...]) * scale
    kvn = _rms(kva_ref[...], kvg_ref[...]).astype(BF16)
    kn = _dot(kvn, wk_ref[...])
    v_ref[...] = _dot(kvn, wv_ref[...]).astype(v_ref.dtype)
    kpe = _rope128(kpe_ref[...], cosv, sinv).astype(k_ref.dtype)
    for h in range(MLA_HEADS):
        lo = h * MLA_HEAD_W
        q_ref[:, lo:lo + LANES] = q[:, lo:lo + LANES].astype(q_ref.dtype)
        q_ref[:, lo + LANES:lo + MLA_HEAD_W] = _rope128(q[:, lo + LANES:lo + MLA_HEAD_W], cosv, sinv).astype(q_ref.dtype)
        k_ref[:, lo:lo + LANES] = kn[:, h * MLA_NOPE:(h + 1) * MLA_NOPE].astype(k_ref.dtype)
        k_ref[:, lo + LANES:lo + MLA_HEAD_W] = kpe


def _mla_prep(p_small, cosv, sinv, q_norm_g, kv_norm_g, wq, wk, wv, tm=256):
    s = p_small.shape[0]
    tm = min(tm, s)
    row = lambda wdt, blk: pl.BlockSpec((tm, wdt), lambda i: (i, blk))
    full = lambda a: pl.BlockSpec(a.shape, lambda i: (0, 0))
    qg = q_norm_g.reshape(1, MLA_Q_LORA)
    kvg = kv_norm_g.reshape(1, MLA_KV_LORA)
    return pl.pallas_call(
        _mla_prep_body,
        grid=(s // tm,),
        in_specs=[row(MLA_Q_LORA, 0), row(MLA_KV_LORA, MLA_Q_LORA // MLA_KV_LORA),
                  row(LANES, (MLA_Q_LORA + MLA_KV_LORA) // LANES), row(LANES, 0), row(LANES, 0),
                  full(qg), full(kvg), full(wq), full(wk), full(wv)],
        out_specs=[row(MLA_HEADS * MLA_HEAD_W, 0), row(MLA_HEADS * MLA_HEAD_W, 0), row(MLA_W, 0)],
        out_shape=[jax.ShapeDtypeStruct((s, MLA_HEADS * MLA_HEAD_W), BF16),
                   jax.ShapeDtypeStruct((s, MLA_HEADS * MLA_HEAD_W), BF16),
                   jax.ShapeDtypeStruct((s, MLA_W), BF16)],
        compiler_params=_cparams(("parallel",)),
        name="mla_prep",
    )(p_small, p_small, p_small, cosv, sinv, qg, kvg, wq, wk, wv)


def _flash_body(qi_ref, ki_ref, q_ref, k_ref, v_ref, o_ref, m_ref, l_ref, acc_ref, *, tq, tk):
    p = pl.program_id(1)
    qi, ki = qi_ref[p], ki_ref[p]

    @pl.when(ki == 0)
    def _():
        m_ref[...] = jnp.full_like(m_ref, -jnp.inf)
        l_ref[...] = jnp.zeros_like(l_ref)
        acc_ref[...] = jnp.zeros_like(acc_ref)

    def update(masked):
        sc = _dot_nt(q_ref[...], k_ref[...])
        if masked:
            row = lax.broadcasted_iota(jnp.int32, (tq, tk), 0)
            col = lax.broadcasted_iota(jnp.int32, (tq, tk), 1)
            sc = jnp.where(col <= row, sc, -jnp.inf)
        m_prev = m_ref[...]
        m_new = jnp.maximum(m_prev, jnp.max(sc, axis=-1, keepdims=True))
        alpha = jnp.exp(m_prev - m_new)
        pr = jnp.exp(sc - m_new)
        l_ref[...] = alpha * l_ref[...] + jnp.sum(pr, axis=-1, keepdims=True)
        acc_ref[...] = alpha * acc_ref[...] + _dot(pr.astype(BF16), v_ref[...])
        m_ref[...] = m_new

    @pl.when(ki < qi)
    def _():
        update(False)

    @pl.when(ki == qi)
    def _():
        update(True)
        o_ref[...] = (acc_ref[...] / l_ref[...]).astype(o_ref.dtype)


def _flash(q, k, v, tile=1024):
    s = q.shape[0]
    t = min(tile, s)
    nb = s // t
    pairs = [(i, j) for i in range(nb) for j in range(i + 1)]
    qi = jnp.asarray([a for a, _ in pairs], jnp.int32)
    ki = jnp.asarray([b for _, b in pairs], jnp.int32)
    return pl.pallas_call(
        functools.partial(_flash_body, tq=t, tk=t),
        grid_spec=pltpu.PrefetchScalarGridSpec(
            num_scalar_prefetch=2,
            grid=(MLA_HEADS, len(pairs)),
            in_specs=[pl.BlockSpec((t, MLA_HEAD_W), lambda h, p, qi, ki: (qi[p], h)),
                      pl.BlockSpec((t, MLA_HEAD_W), lambda h, p, qi, ki: (ki[p], h)),
                      pl.BlockSpec((t, MLA_DV), lambda h, p, qi, ki: (ki[p], h))],
            out_specs=pl.BlockSpec((t, MLA_DV), lambda h, p, qi, ki: (qi[p], h)),
            scratch_shapes=[pltpu.VMEM((t, 1), F32), pltpu.VMEM((t, 1), F32), pltpu.VMEM((t, MLA_DV), F32)]),
        out_shape=jax.ShapeDtypeStruct((s, MLA_W), BF16),
        compiler_params=_cparams(("parallel", "arbitrary")),
        name="mla_flash",
    )(qi, ki, q, k, v)


def _merge_body(ro_ref, go_ref, mo_ref, wr_ref, wg_ref, wm_ref, gr_ref, gg_ref, gm_ref, o_ref):
    acc = _sigmoid(gr_ref[...]) * _dot(ro_ref[...], wr_ref[...])
    acc = acc + _sigmoid(gg_ref[...]) * _dot(go_ref[...], wg_ref[...])
    acc = acc + _sigmoid(gm_ref[...]) * _dot(mo_ref[...], wm_ref[...])
    o_ref[...] = acc.astype(o_ref.dtype)


def _merge(ro, go, mo, wr, wg, wm, gates, tm=512, tn=512):
    s, d = ro.shape[0], wr.shape[1]
    tm, tn = min(tm, s), min(tn, d)
    nb = d // tn
    act = lambda a: pl.BlockSpec((tm, a.shape[1]), lambda i, j: (i, 0))
    wsp = lambda a: pl.BlockSpec((a.shape[0], tn), lambda i, j: (0, j))
    gsp = lambda b: pl.BlockSpec((tm, tn), lambda i, j: (i, b * nb + j))
    return pl.pallas_call(
        _merge_body,
        grid=(s // tm, nb),
        in_specs=[act(ro), act(go), act(mo), wsp(wr), wsp(wg), wsp(wm), gsp(0), gsp(1), gsp(2)],
        out_specs=pl.BlockSpec((tm, tn), lambda i, j: (i, j)),
        out_shape=jax.ShapeDtypeStruct((s, d), BF16),
        compiler_params=_cparams(("parallel", "parallel")),
        name="merge",
    )(ro, go, mo, wr, wg, wm, gates, gates, gates)


def _layer_norm(y, g, b):
    mu = jnp.mean(y, axis=-1, keepdims=True)
    d = y - mu
    var = jnp.mean(d * d, axis=-1, keepdims=True)
    return d * lax.rsqrt(var + LN_EPS) * g + b


def _ln_router_body(h_ref, mix_ref, g_ref, b_ref, wr_ref, br_ref, o_ref, lg_ref):
    y = _layer_norm(DEEPNORM_ALPHA * h_ref[...] + mix_ref[...], g_ref[...], b_ref[...])
    o_ref[...] = y
    lg_ref[...] = _dot(y, wr_ref[...], HIGHEST) + br_ref[...]


def _ln_router(h, mix, g, b, w_router, b_router, tm=256):
    s, d = h.shape
    tm = min(tm, s)
    row = lambda wdt: pl.BlockSpec((tm, wdt), lambda i: (i, 0))
    full = lambda a: pl.BlockSpec(a.shape, lambda i: (0, 0))
    g, b = g.reshape(1, d), b.reshape(1, d)
    return pl.pallas_call(
        _ln_router_body,
        grid=(s // tm,),
        in_specs=[row(d), row(d), full(g), full(b), full(w_router), full(b_router)],
        out_specs=[row(d), row(LANES)],
        out_shape=[jax.ShapeDtypeStruct((s, d), F32), jax.ShapeDtypeStruct((s, LANES), F32)],
        compiler_params=_cparams(("parallel",)),
        name="ln_router",
    )(h, mix, g, b, w_router, b_router)


def _row_gather(idx_ref, base, src_hbm, dst, sem, n_rows):
    def issue(r, carry):
        tok = idx_ref[base + r]
        pltpu.make_async_copy(src_hbm.at[pl.ds(tok, 1), :], dst.at[pl.ds(r, 1), :], sem).start()
        return carry
    lax.fori_loop(0, n_rows, issue, 0)


def _row_gather_wait(src_hbm, dst, sem, n_rows):
    pltpu.make_async_copy(src_hbm.at[pl.ds(0, n_rows), :], dst, sem).wait()


def _moe_body(be_ref, nu_ref, tok_ref, x_hbm, wg_ref, wu_ref, wd_ref, o_ref, xbuf, sems):
    b = pl.program_id(0)
    n_used = nu_ref[0]
    slot = lax.rem(b, 2)

    @pl.when(b == 0)
    def _():
        _row_gather(tok_ref, 0, x_hbm, xbuf.at[0], sems.at[0], MOE_ROWS)

    @pl.when(b < n_used)
    def _():
        _row_gather_wait(x_hbm, xbuf.at[slot], sems.at[slot], MOE_ROWS)

        @pl.when(b + 1 < n_used)
        def _():
            _row_gather(tok_ref, (b + 1) * MOE_ROWS, x_hbm, xbuf.at[1 - slot], sems.at[1 - slot], MOE_ROWS)

        xb = xbuf[slot].astype(BF16)
        hdn = _silu(_dot(xb, wg_ref[0])) * _dot(xb, wu_ref[0])
        o_ref[...] = _dot(hdn.astype(BF16), wd_ref[0])

    @pl.when(b >= n_used)
    def _():
        o_ref[...] = jnp.zeros_like(o_ref)


def _moe_experts(x, block_expert, n_used, buf_tok, w_gate, w_up, w_down):
    n, d = x.shape
    n_blocks = block_expert.shape[0]
    wspec = lambda a: pl.BlockSpec((1,) + a.shape[1:], lambda b, be, nu, tok: (be[b], 0, 0))
    return pl.pallas_call(
        _moe_body,
        grid_spec=pltpu.PrefetchScalarGridSpec(
            num_scalar_prefetch=3,
            grid=(n_blocks,),
            in_specs=[pl.BlockSpec(memory_space=pl.ANY), wspec(w_gate), wspec(w_up), wspec(w_down)],
            out_specs=pl.BlockSpec((MOE_ROWS, d), lambda b, be, nu, tok: (b, 0)),
            scratch_shapes=[pltpu.VMEM((2, MOE_ROWS, d), F32), pltpu.SemaphoreType.DMA((2,))]),
        out_shape=jax.ShapeDtypeStruct((n_blocks * MOE_ROWS, d), F32),
        compiler_params=_cparams(("arbitrary",)),
        name="moe_experts",
    )(block_expert, n_used, buf_tok, x, w_gate, w_up, w_down)


def _combine_body(pos_ref, h_ref, wt_ref, g_ref, b_ref, y_hbm, o_ref, ob_ref, ybuf, sems, *, tm):
    i = pl.program_id(0)
    nsteps = pl.num_programs(0)
    slot = lax.rem(i, 2)

    def gather(step, sl):
        for k in range(TOP_K):
            _row_gather(pos_ref, (k * nsteps + step) * tm, y_hbm, ybuf.at[sl, k], sems.at[sl], tm)

    @pl.when(i == 0)
    def _():
        gather(0, 0)

    for k in range(TOP_K):
        _row_gather_wait(y_hbm, ybuf.at[slot, k], sems.at[slot], tm)

    @pl.when(i + 1 < nsteps)
    def _():
        gather(i + 1, 1 - slot)

    wt = wt_ref[...]
    ffn = ybuf[slot, 0] * wt[:, 0:1] + ybuf[slot, 1] * wt[:, 1:2]
    y = _layer_norm(DEEPNORM_ALPHA * h_ref[...] + ffn, g_ref[...], b_ref[...])
    o_ref[...] = y
    ob_ref[...] = y.astype(ob_ref.dtype)


def _combine_ln(h, yb, pos, wts, g, b, tm=128):
    s, d = h.shape
    tm = min(tm, s)
    row = lambda wdt: pl.BlockSpec((tm, wdt), lambda i, pos: (i, 0))
    full = lambda a: pl.BlockSpec(a.shape, lambda i, pos: (0, 0))
    g, b = g.reshape(1, d), b.reshape(1, d)
    return pl.pallas_call(
        functools.partial(_combine_body, tm=tm),
        grid_spec=pltpu.PrefetchScalarGridSpec(
            num_scalar_prefetch=1,
            grid=(s // tm,),
            in_specs=[row(d), row(TOP_K), full(g), full(b), pl.BlockSpec(memory_space=pl.ANY)],
            out_specs=[row(d), row(d)],
            scratch_shapes=[pltpu.VMEM((2, TOP_K, tm, d), F32), pltpu.SemaphoreType.DMA((2,))]),
        out_shape=[jax.ShapeDtypeStruct((s, d), F32), jax.ShapeDtypeStruct((s, d), BF16)],
        compiler_params=_cparams(("arbitrary",)),
        name="combine_ln",
    )(pos, h, wts, g, b, yb)


def _route(logits):
    n = logits.shape[0]
    group_prob = jax.nn.softmax(logits[:, :N_GROUPS], axis=-1)
    p_group, g_sel = lax.top_k(group_prob, 1)
    expert_logits = logits[:, N_GROUPS:N_GROUPS + N_EXPERTS].reshape(n, N_GROUPS, EXPERTS_PER_GROUP)
    in_group = jnp.take_along_axis(expert_logits, g_sel[:, :, None], axis=1)[:, 0]
    top_logit, top_local = lax.top_k(in_group, TOP_K)
    weights = p_group * jax.nn.softmax(top_logit, axis=-1)
    expert_id = (g_sel * EXPERTS_PER_GROUP + top_local).astype(jnp.int32)

    a = n * TOP_K
    flat_e = expert_id.T.reshape(a)
    flat_tok = jnp.tile(jnp.arange(n, dtype=jnp.int32), TOP_K)
    onehot = (flat_e[:, None] == jnp.arange(N_EXPERTS, dtype=jnp.int32)[None, :]).astype(jnp.int32)
    csum = jnp.cumsum(onehot, axis=0)
    counts = csum[-1]
    rank = jnp.take_along_axis(csum, flat_e[:, None], axis=1)[:, 0] - 1
    padded = (counts + MOE_ROWS - 1) // MOE_ROWS * MOE_ROWS
    pad_end = jnp.cumsum(padded)
    pad_start = pad_end - padded
    pos = (pad_start[flat_e] + rank).astype(jnp.int32)
    n_blocks = a // MOE_ROWS + N_EXPERTS
    buf_tok = jnp.zeros((n_blocks * MOE_ROWS,), jnp.int32).at[pos].set(flat_tok)
    block_start = jnp.arange(n_blocks, dtype=jnp.int32) * MOE_ROWS
    block_expert = jnp.minimum(jnp.searchsorted(pad_end, block_start, side='right'), N_EXPERTS - 1).astype(jnp.int32)
    n_used = (pad_end[-1] // MOE_ROWS).astype(jnp.int32).reshape(1)
    return block_expert, n_used, buf_tok, pos, weights


def _rope_tables(positions, dim, half_stride):
    n = dim // 2
    inv_freq = ROPE_THETA ** (-jnp.arange(0, dim, 2, dtype=F32) / dim)
    ang = positions.astype(F32)[:, None] * inv_freq
    cos, sin = jnp.cos(ang), jnp.sin(ang)
    pad = jnp.zeros((positions.shape[0], half_stride - n), F32)
    return (jnp.concatenate([cos, pad, cos, pad], axis=-1), jnp.concatenate([-sin, pad, sin, pad], axis=-1))


def _spread_rope_cols(w):
    half = MLA_ROPE // 2
    z = jnp.zeros(w.shape[:-1] + (LANES // 2 - half,), w.dtype)
    return jnp.concatenate([w[..., :half], z, w[..., half:], z], axis=-1)


def kernel(x, positions, w_in, gdn_conv_w, gdn_a_log, gdn_dt_bias, gdn_norm_g, mla_q_norm_g, mla_w_qb,
           mla_kv_norm_g, mla_w_kvb, w_br_ret, w_br_gdn, w_br_mla, w_out, ln1_g, ln1_b, w_group, b_group,
           w_expert, b_expert, w_gate, w_up, w_down, ln2_g, ln2_b):
    batch, s, d = x.shape
    assert batch == 1
    pos = positions[0]
    cos_ret, sin_ret = _rope_tables(pos, RET_DK, RET_DK // 2)
    cos_mla, sin_mla = _rope_tables(pos, MLA_ROPE, LANES // 2)
    log_gamma = jnp.log(1.0 - 2.0 ** (-5.0 - jnp.arange(RET_HEADS, dtype=F32)))

    o_ret = 0
    o_gdn = o_ret + 2 * RET_HEADS * RET_DK + 2 * RET_W
    o_z = o_gdn + GDN_QKV
    o_a = o_z + GDN_W
    o_b = o_a + GDN_HEADS
    o_qa = o_b + GDN_HEADS
    o_kva = o_qa + MLA_Q_LORA
    o_kpe = o_kva + MLA_KV_LORA
    o_gates = o_kpe + MLA_ROPE
    ab_block = (MLA_Q_LORA + MLA_KV_LORA + LANES) // LANES

    h = x[0]
    hb = h.astype(BF16)
    for l in range(w_in.shape[0]):
        wl = w_in[l]
        w_small = jnp.concatenate(
            [wl[:, o_qa:o_kpe], _spread_rope_cols(wl[:, o_kpe:o_gates]), wl[:, o_a:o_qa],
             jnp.zeros((d, LANES - 2 * GDN_HEADS), F32)], axis=1).astype(BF16)
        p_ret = _matmul(hb, wl[:, o_ret:o_gdn].astype(BF16), F32)
        p_gdn = _matmul(hb, wl[:, o_gdn:o_z].astype(BF16), F32)
        p_z = _matmul(hb, wl[:, o_z:o_a].astype(BF16), F32)
        p_small = _matmul(hb, w_small, F32, tn=w_small.shape[1] // 2)
        p_gates = _matmul(hb, wl[:, o_gates:].astype(BF16), F32)

        ro = _retention(p_ret, cos_ret, sin_ret, log_gamma)
        go = _gdn(p_gdn, p_z, p_small, ab_block, gdn_conv_w[l], gdn_a_log[l], gdn_dt_bias[l], gdn_norm_g[l])

        wq = mla_w_qb[l].reshape(MLA_Q_LORA, MLA_HEADS, MLA_NOPE + MLA_ROPE)
        wq = jnp.concatenate([wq[..., :MLA_NOPE], _spread_rope_cols(wq[..., MLA_NOPE:])], axis=-1)
        wq = wq.reshape(MLA_Q_LORA, MLA_HEADS * MLA_HEAD_W).astype(BF16)
        wkv = mla_w_kvb[l].reshape(MLA_KV_LORA, MLA_HEADS, MLA_NOPE + MLA_DV)
        wk = wkv[..., :MLA_NOPE].reshape(MLA_KV_LORA, MLA_HEADS * MLA_NOPE).astype(BF16)
        wv = wkv[..., MLA_NOPE:].reshape(MLA_KV_LORA, MLA_W).astype(BF16)
        q_full, k_full, v_mla = _mla_prep(p_small, cos_mla, sin_mla, mla_q_norm_g[l], mla_kv_norm_g[l], wq, wk, wv)
        mo = _flash(q_full, k_full, v_mla)

        merged = _merge(ro, go, mo, w_br_ret[l].astype(BF16), w_br_gdn[l].astype(BF16),
                        w_br_mla[l].astype(BF16), p_gates)
        mix = _matmul(merged, w_out[l].astype(BF16), F32)

        w_router = jnp.concatenate(
            [w_group[l], w_expert[l], jnp.zeros((d, LANES - N_GROUPS - N_EXPERTS), F32)], axis=1)
        b_router = jnp.concatenate(
            [b_group[l], b_expert[l], jnp.zeros((LANES - N_GROUPS - N_EXPERTS,), F32)]).reshape(1, LANES)
        h1, logits = _ln_router(h, mix, ln1_g[l], ln1_b[l], w_router, b_router)

        block_expert, n_used, buf_tok, pos_rows, weights = _route(logits)
        yb = _moe_experts(h1, block_expert, n_used, buf_tok, w_gate[l].astype(BF16), w_up[l].astype(BF16),
                          w_down[l].astype(BF16))
        h, hb = _combine_ln(h1, yb, pos_rows, weights, ln2_g[l], ln2_b[l])
    return h[None]
```
